```python
import jax, jax.numpy as jnp
from jax import lax
import numpy as np

D_MODEL = 1024
BATCH = 2
SEQ = 8192
DEPTH = 4

POOL_WINDOWS = (2, 4, 8, 16)
POOL_GROUPS = 4
POOL_GC = 64
POOL_W = POOL_GROUPS * POOL_GC
M_HEADS = 4
M_DH = 64
M_W = M_HEADS * M_DH
M_CONV = 4
M_CHUNK = 128
A_HEADS = 8
A_DH = 64
A_W = A_HEADS * A_DH
MOBA_BLOCK = 256
MOBA_TOPK = 3
Q_BLOCK = 128
MIX_W = POOL_W + M_W + A_W
OFF_MQK = POOL_W
OFF_MV = OFF_MQK + 2 * M_W
OFF_MO = OFF_MV + M_W
OFF_MI = OFF_MO + M_W
OFF_MF = OFF_MI + M_HEADS
OFF_A = OFF_MF + M_HEADS
IN_W = OFF_A + 3 * A_W
D_FF = 2816
FFN_CONV = 3
EPS = 1e-6

kernel_name = 'hybrid_pool_mlstm_moba_convffn'


def rmsnorm(x, g):
    xf = x.astype(jnp.float32)
    y = xf * lax.rsqrt(jnp.mean(xf * xf, axis=-1, keepdims=True) + EPS) * g.astype(jnp.float32)
    return y.astype(x.dtype)


def causal_dwconv(x, w):
    K = w.shape[0]
    S = x.shape[1]
    xp = jnp.pad(x, ((0, 0), (K - 1, 0), (0, 0)))
    y = xp[:, 0:S] * w[0]
    for j in range(1, K):
        y = y + xp[:, j:j + S] * w[j]
    return y


def pool_mixer(xp, w_pool, scale):
    B, S, _ = xp.shape
    xg = xp.astype(jnp.float32).reshape(B, S, POOL_GROUPS, POOL_GC)
    cs = jnp.concatenate([jnp.zeros((B, 1, POOL_GROUPS, POOL_GC), jnp.float32), jnp.cumsum(xg, axis=1)], axis=1)
    t = jnp.arange(S)
    outs = []
    for g, w in enumerate(POOL_WINDOWS):
        lo = jnp.maximum(t + 1 - w, 0)
        cnt = (t + 1 - lo).astype(jnp.float32)
        win_sum = cs[:, 1:, g] - cs[:, lo, g]
        outs.append(win_sum / cnt[None, :, None] - xg[:, :, g])
    d = jnp.stack(outs, axis=2)
    y = jnp.einsum('bsgc,gcd->bsgd', d, w_pool.astype(jnp.float32)).reshape(B, S, POOL_W)
    return y * scale.astype(jnp.float32)


def mlstm_chunkwise(q, k, v, i_pre, f_pre):
    B, S, H, Dh = q.shape
    L = M_CHUNK
    NC = S // L
    def ch5(t):
        return t.reshape(B, NC, L, H, Dh).transpose(1, 0, 3, 2, 4)
    def ch4(t):
        return t.reshape(B, NC, L, H).transpose(1, 0, 3, 2)
    logf = jax.nn.log_sigmoid(f_pre)
    causal = jnp.tril(jnp.ones((L, L), dtype=bool))

    def step(carry, inp):
        C, n, m = carry
        qc, kc, vc, ic, lf = inp
        g = jnp.cumsum(lf, axis=-1)
        logd = jnp.where(causal, g[..., :, None] - g[..., None, :] + ic[..., None, :], -jnp.inf)
        inter = g + m[..., None]
        mt = jnp.maximum(inter, jnp.max(logd, axis=-1))
        w = jnp.einsum('bhtd,bhsd->bhts', qc, kc) * jnp.exp(logd - mt[..., None])
        a = jnp.exp(inter - mt)
        num = jnp.einsum('bhts,bhse->bhte', w, vc) + a[..., None] * jnp.einsum('bhtd,bhde->bhte', qc, C)
        den = jnp.sum(w, axis=-1) + a * jnp.einsum('bhtd,bhd->bht', qc, n)
        hc = num / jnp.maximum(jnp.abs(den), jnp.exp(-mt))[..., None]
        gl = g[..., -1]
        wl = gl[..., None] - g + ic
        m_new = jnp.maximum(gl + m, jnp.max(wl, axis=-1))
        wk = jnp.exp(wl - m_new[..., None])
        dec = jnp.exp(gl + m - m_new)
        C = dec[..., None, None] * C + jnp.einsum('bhs,bhsd,bhse->bhde', wk, kc, vc)
        n = dec[..., None] * n + jnp.einsum('bhs,bhsd->bhd', wk, kc)
        return (C, n, m_new), hc

    init = (jnp.zeros((B, H, Dh, Dh), jnp.float32), jnp.zeros((B, H, Dh), jnp.float32), jnp.zeros((B, H), jnp.float32))
    _, hs = lax.scan(step, init, (ch5(q), ch5(k), ch5(v), ch4(i_pre), ch4(logf)))
    return hs.transpose(1, 0, 3, 2, 4).reshape(B, S, H, Dh)


def moba_attention(q, k, v):
    q = q.astype(jnp.float32).transpose(0, 2, 1, 3)
    k = k.astype(jnp.float32).transpose(0, 2, 1, 3)
    v = v.astype(jnp.float32).transpose(0, 2, 1, 3)
    B, H, S, Dh = q.shape
    NB = -(-S // MOBA_BLOCK)
    pad = NB * MOBA_BLOCK - S
    kp = jnp.pad(k, ((0, 0), (0, 0), (0, pad), (0, 0)))
    vp = jnp.pad(v, ((0, 0), (0, 0), (0, pad), (0, 0)))
    kb = kp.reshape(B, H, NB, MOBA_BLOCK, Dh)
    vb = vp.reshape(B, H, NB, MOBA_BLOCK, Dh)
    kmean = jnp.mean(kb, axis=3)
    topk = min(MOBA_TOPK, NB)
    NQ = S // Q_BLOCK
    scale = A_DH ** -0.5
    qs = q.reshape(B, H, NQ, Q_BLOCK, Dh).transpose(2, 0, 1, 3, 4)
    bi = jnp.arange(B)[:, None, None, None]
    hi = jnp.arange(H)[None, :, None, None]
    blk_ids = jnp.arange(NB)

    def one(args):
        qi, ci = args
        q0 = ci * Q_BLOCK
        own = q0 // MOBA_BLOCK
        gate = jnp.einsum('bhqd,bhnd->bhqn', qi, kmean)
        gate = jnp.where((blk_ids < own)[None, None, None, :], gate, -jnp.inf)
        gval, gidx = lax.top_k(gate, topk)
        valid = jnp.isfinite(gval)
        ksel = kb[bi, hi, gidx]
        vsel = vb[bi, hi, gidx]
        s_sel = jnp.einsum('bhqd,bhqjkd->bhqjk', qi, ksel) * scale
        s_sel = jnp.where(valid[..., None], s_sel, -jnp.inf).reshape(B, H, Q_BLOCK, topk * MOBA_BLOCK)
        k_own = lax.dynamic_slice_in_dim(kp, own * MOBA_BLOCK, MOBA_BLOCK, axis=2)
        v_own = lax.dynamic_slice_in_dim(vp, own * MOBA_BLOCK, MOBA_BLOCK, axis=2)
        s_own = jnp.einsum('bhqd,bhkd->bhqk', qi, k_own) * scale
        kpos = own * MOBA_BLOCK + jnp.arange(MOBA_BLOCK)
        qpos = q0 + jnp.arange(Q_BLOCK)
        s_own = jnp.where((kpos[None, :] <= qpos[:, None])[None, None], s_own, -jnp.inf)
        p = jax.nn.softmax(jnp.concatenate([s_sel, s_own], axis=-1), axis=-1)
        p_sel = p[..., :topk * MOBA_BLOCK].reshape(B, H, Q_BLOCK, topk, MOBA_BLOCK)
        p_own = p[..., topk * MOBA_BLOCK:]
        return jnp.einsum('bhqjk,bhqjkd->bhqd', p_sel, vsel) + jnp.einsum('bhqk,bhkd->bhqd', p_own, v_own)

    outs = lax.map(one, (qs, jnp.arange(NQ)))
    return outs.transpose(1, 0, 3, 2, 4).reshape(B, S, H, Dh)


def setup_inputs(seed: int = 0) -> dict:
    key = jax.random.key(seed)
    ks = jax.random.split(key, 18)
    nrm = jax.random.normal
    f32 = jnp.float32
    return {
        'x': nrm(ks[0], (BATCH, SEQ, D_MODEL), f32),
        'ln1_g': 1.0 + 0.02 * nrm(ks[1], (DEPTH, D_MODEL), f32),
        'w_in': nrm(ks[2], (DEPTH, D_MODEL, IN_W), f32) * D_MODEL ** -0.5,
        'pool_w': nrm(ks[3], (DEPTH, POOL_GROUPS, POOL_GC, POOL_GC), f32) * POOL_GC ** -0.5,
        'pool_scale': 1.0 + 0.1 * nrm(ks[4], (DEPTH, POOL_W), f32),
        'm_conv': nrm(ks[5], (DEPTH, M_CONV, 2 * M_W), f32) * M_CONV ** -0.5,
        'm_b_i': 0.1 * nrm(ks[6], (DEPTH, M_HEADS), f32),
        'm_b_f': jnp.linspace(3.0, 6.0, M_HEADS, dtype=f32)[None, :] + 0.01 * nrm(ks[7], (DEPTH, M_HEADS), f32),
        'm_norm_g': 1.0 + 0.02 * nrm(ks[8], (DEPTH, M_W), f32),
        'a_q_g': 1.0 + 0.02 * nrm(ks[9], (DEPTH, A_DH), f32),
        'a_k_g': 1.0 + 0.02 * nrm(ks[10], (DEPTH, A_DH), f32),
        'w_out': nrm(ks[11], (DEPTH, MIX_W, D_MODEL), f32) * MIX_W ** -0.5,
        'ln2_g': 1.0 + 0.02 * nrm(ks[12], (DEPTH, D_MODEL), f32),
        'w_up': nrm(ks[13], (DEPTH, D_MODEL, 2 * D_FF), f32) * D_MODEL ** -0.5,
        'ffn_conv': nrm(ks[14], (DEPTH, FFN_CONV, 2 * D_FF), f32) * FFN_CONV ** -0.5,
        'w_down': nrm(ks[15], (DEPTH, D_FF, D_MODEL), f32) * D_FF ** -0.5,
    }


def reference(x, ln1_g, w_in, pool_w, pool_scale, m_conv, m_b_i, m_b_f, m_norm_g,
              a_q_g, a_k_g, w_out, ln2_g, w_up, ffn_conv, w_down):
    B, S, _ = x.shape
    for l in range(DEPTH):
        h = rmsnorm(x, ln1_g[l])
        z = h @ w_in[l]
        po = pool_mixer(z[..., :OFF_MQK], pool_w[l], pool_scale[l])
        qk = jax.nn.silu(causal_dwconv(z[..., OFF_MQK:OFF_MV], m_conv[l])).astype(jnp.float32)
        mq = qk[..., :M_W].reshape(B, S, M_HEADS, M_DH)
        mk = qk[..., M_W:].reshape(B, S, M_HEADS, M_DH) * (M_DH ** -0.5)
        mv = z[..., OFF_MV:OFF_MO].astype(jnp.float32).reshape(B, S, M_HEADS, M_DH)
        mo = jax.nn.sigmoid(z[..., OFF_MO:OFF_MI])
        mi = (z[..., OFF_MI:OFF_MF] + m_b_i[l]).astype(jnp.float32)
        mf = (z[..., OFF_MF:OFF_A] + m_b_f[l]).astype(jnp.float32)
        hm = mlstm_chunkwise(mq, mk, mv, mi, mf)
        hm = rmsnorm(hm, m_norm_g[l].reshape(M_HEADS, M_DH)).reshape(B, S, M_W).astype(x.dtype) * mo
        aq = rmsnorm(z[..., OFF_A:OFF_A + A_W].reshape(B, S, A_HEADS, A_DH), a_q_g[l])
        ak = rmsnorm(z[..., OFF_A + A_W:OFF_A + 2 * A_W].reshape(B, S, A_HEADS, A_DH), a_k_g[l])
        av = z[..., OFF_A + 2 * A_W:].reshape(B, S, A_HEADS, A_DH)
        ao = moba_attention(aq, ak, av).reshape(B, S, A_W)
        mix = jnp.concatenate([po.astype(x.dtype), hm, ao.astype(x.dtype)], axis=-1)
        x = x + mix @ w_out[l]
        h2 = rmsnorm(x, ln2_g[l])
        u = causal_dwconv(h2 @ w_up[l], ffn_conv[l])
        x = x + (jax.nn.silu(u[..., :D_FF]) * u[..., D_FF:]) @ w_down[l]
    return x
```

```python
import functools

import jax
import jax.numpy as jnp
from jax import lax
from jax.experimental import pallas as pl
from jax.experimental.pallas import tpu as pltpu

F32 = jnp.float32
BF16 = jnp.bfloat16

D_MODEL = 1024
DEPTH = 4
POOL_WINDOWS = (2, 4, 8, 16)
POOL_GC = 64
POOL_W = 256
M_HEADS = 4
M_DH = 64
M_W = 256
M_CONV = 4
A_HEADS = 8
A_DH = 64
A_W = 512
MOBA_BLOCK = 256
MOBA_TOPK = 3
D_FF = 2816
FFN_CONV = 3
EPS = 1e-6

OFF_MI = POOL_W + 4 * M_W
OFF_MF = OFF_MI + M_HEADS
OFF_A = OFF_MF + M_HEADS

C_POOL = 0
C_MQK = 256
C_MV = 768
C_MO = 1024
C_AQ = 1280
C_AK = 1792
C_AV = 2304
C_MI = 2816
C_MF = 2944
IN_WP = 3072

LANES = 128
VMEM_LIMIT = 48 * 1024 * 1024

TM_IN = 512
TP_POOL = 512
POOL_HALO = 16
L_CHUNK = 128
TQ = MOBA_BLOCK
TM_FFN = 512
TF_FFN = 256
NF_FFN = D_FF // TF_FFN

NEG_INF = float("-inf")


def _dot(a, b):
    return jnp.dot(a, b, preferred_element_type=F32)


def _dot_f32(a, b):
    return jnp.dot(a, b, preferred_element_type=F32, precision=lax.Precision.HIGHEST)


def _group_sumsq(z, ones_bd):
    sq = z * z
    hi = sq.astype(BF16)
    lo = (sq - hi.astype(F32)).astype(BF16)
    return _dot(hi, ones_bd) + _dot(lo, ones_bd)


def _log_sigmoid(x):
    return jnp.minimum(x, 0.0) - jnp.log1p(jnp.exp(-jnp.abs(x)))


def _sigmoid(x):
    return 1.0 / (1.0 + jnp.exp(-x))


def _in_proj_kernel(x_ref, g_ref, w_ref, bif_ref, gq_ref, gk_ref, ones_ref,
                    zp_ref, zqk_ref, zv_ref, zo_ref, aq_ref, ak_ref, av_ref, zif_ref, kmean_ref):
    xf = x_ref[...]
    ms = jnp.mean(xf * xf, axis=-1, keepdims=True)
    h = (xf * lax.rsqrt(ms + EPS) * g_ref[...]).astype(BF16)

    def proj(a, b):
        return _dot(h, w_ref[:, a:b])

    zp_ref[...] = proj(C_POOL, C_MQK)
    zqk_ref[...] = proj(C_MQK, C_MV)
    zv_ref[...] = proj(C_MV, C_MO)
    zo_ref[...] = _sigmoid(proj(C_MO, C_AQ))

    ones_bd = ones_ref[...]
    zq = proj(C_AQ, C_AK)
    aq_ref[...] = zq * lax.rsqrt(_group_sumsq(zq, ones_bd) * (1.0 / A_DH) + EPS) * gq_ref[...]
    zk = proj(C_AK, C_AV)
    ak = zk * lax.rsqrt(_group_sumsq(zk, ones_bd) * (1.0 / A_DH) + EPS) * gk_ref[...]
    ak_ref[...] = ak.astype(BF16)
    for blk in range(TM_IN // MOBA_BLOCK):
        kmean_ref[0, blk:blk + 1, :] = jnp.mean(
            ak[blk * MOBA_BLOCK:(blk + 1) * MOBA_BLOCK, :], axis=0, keepdims=True)
    av_ref[...] = proj(C_AV, C_MI)

    zif = proj(C_MI, IN_WP) + bif_ref[...]
    zif_ref[:, :LANES] = zif[:, :LANES]
    zif_ref[:, LANES:] = _log_sigmoid(zif[:, LANES:])


def _in_proj(x2, ln1_g, w_in_p, bif, gq, gk, ones_bd, layer):
    m = x2.shape[0]
    grid = (m // TM_IN,)
    row = lambda i: (i, 0)
    const = lambda i: (0, 0)
    out_shape = (
        jax.ShapeDtypeStruct((m, POOL_W), F32),
        jax.ShapeDtypeStruct((m, 2 * M_W), F32),
        jax.ShapeDtypeStruct((m, M_W), F32),
        jax.ShapeDtypeStruct((m, M_W), F32),
        jax.ShapeDtypeStruct((m, A_W), F32),
        jax.ShapeDtypeStruct((m, A_W), BF16),
        jax.ShapeDtypeStruct((m, A_W), F32),
        jax.ShapeDtypeStruct((m, 2 * LANES), F32),
        jax.ShapeDtypeStruct((m // TM_IN, TM_IN // MOBA_BLOCK, A_W), F32),
    )
    out_specs = (
        pl.BlockSpec((TM_IN, POOL_W), row),
        pl.BlockSpec((TM_IN, 2 * M_W), row),
        pl.BlockSpec((TM_IN, M_W), row),
        pl.BlockSpec((TM_IN, M_W), row),
        pl.BlockSpec((TM_IN, A_W), row),
        pl.BlockSpec((TM_IN, A_W), row),
        pl.BlockSpec((TM_IN, A_W), row),
        pl.BlockSpec((TM_IN, 2 * LANES), row),
        pl.BlockSpec((1, TM_IN // MOBA_BLOCK, A_W), lambda i: (i, 0, 0)),
    )
    in_specs = [
        pl.BlockSpec((TM_IN, D_MODEL), row),
        pl.BlockSpec((None, 1, D_MODEL), lambda i: (layer, 0, 0)),
        pl.BlockSpec((None, D_MODEL, IN_WP), lambda i: (layer, 0, 0)),
        pl.BlockSpec((None, 1, 2 * LANES), lambda i: (layer, 0, 0)),
        pl.BlockSpec((None, 1, A_W), lambda i: (layer, 0, 0)),
        pl.BlockSpec((None, 1, A_W), lambda i: (layer, 0, 0)),
        pl.BlockSpec((A_W, A_W), const),
    ]
    return pl.pallas_call(
        _in_proj_kernel,
        out_shape=out_shape,
        grid=grid,
        in_specs=in_specs,
        out_specs=out_specs,
        compiler_params=pltpu.CompilerParams(
            dimension_semantics=("arbitrary",), vmem_limit_bytes=VMEM_LIMIT),
        name="in_proj",
    )(x2, ln1_g, w_in_p, bif, gq, gk, ones_bd)


def _pool_kernel(zp_ref, wbd_ref, scale_ref, out_ref, halo_ref, *, tiles_per_seq):
    i = pl.program_id(0)
    first = (i % tiles_per_seq) == 0
    x = zp_ref[...]

    @pl.when(first)
    def _():
        halo_ref[...] = jnp.zeros_like(halo_ref)

    prev = halo_ref[...]
    halo_ref[...] = x[TP_POOL - POOL_HALO:, :]
    xe = jnp.concatenate([prev, x], axis=0)
    s2 = xe + pltpu.roll(xe, 1, 0)
    s4 = s2 + pltpu.roll(s2, 2, 0)
    s8 = s4 + pltpu.roll(s4, 4, 0)
    s16 = s8 + pltpu.roll(s8, 8, 0)
    shape = (TP_POOL, POOL_W)
    group = lax.broadcasted_iota(jnp.int32, shape, 1) // POOL_GC
    pos = (i % tiles_per_seq) * TP_POOL + lax.broadcasted_iota(jnp.int32, shape, 0)
    wsum = s16[POOL_HALO:, :]
    wsize = jnp.full(shape, POOL_WINDOWS[3], jnp.int32)
    for g, s in ((2, s8), (1, s4), (0, s2)):
        wsum = jnp.where(group == g, s[POOL_HALO:, :], wsum)
        wsize = jnp.where(group == g, POOL_WINDOWS[g], wsize)
    cnt = jnp.minimum(pos + 1, wsize).astype(F32)
    d = wsum / cnt - x
    out_ref[...] = _dot(d.astype(BF16), wbd_ref[...]) * scale_ref[...]


def _pool(zp, wbd, scale, layer, seq):
    m = zp.shape[0]
    row = lambda i: (i, 0)
    return pl.pallas_call(
        functools.partial(_pool_kernel, tiles_per_seq=seq // TP_POOL),
        out_shape=jax.ShapeDtypeStruct((m, POOL_W), F32),
        grid=(m // TP_POOL,),
        in_specs=[
            pl.BlockSpec((TP_POOL, POOL_W), row),
            pl.BlockSpec((None, POOL_W, POOL_W), lambda i: (layer, 0, 0)),
            pl.BlockSpec((None, 1, POOL_W), lambda i: (layer, 0, 0)),
        ],
        out_specs=pl.BlockSpec((TP_POOL, POOL_W), row),
        scratch_shapes=[pltpu.VMEM((POOL_HALO, POOL_W), F32)],
        compiler_params=pltpu.CompilerParams(dimension_semantics=("arbitrary",)),
        name="pool",
    )(zp, wbd, scale)


def _mlstm_kernel(zqk_ref, zv_ref, zo_ref, zif_ref, conv_ref, ng_ref, tri_ref, ones_ref, out_ref,
                  halo_ref, c_ref, m_ref):
    L = L_CHUNK
    ci = pl.program_id(1)

    @pl.when(ci == 0)
    def _():
        halo_ref[...] = jnp.zeros_like(halo_ref)
        c_ref[...] = jnp.zeros_like(c_ref)
        m_ref[...] = jnp.zeros_like(m_ref)

    zqk = zqk_ref[...]
    prev = halo_ref[...]
    halo_ref[...] = zqk[L - 8:, :]
    row = lax.broadcasted_iota(jnp.int32, (L, 2 * M_W), 0)
    cw = conv_ref[...]
    y = cw[M_CONV - 1:M_CONV, :] * zqk
    for k in range(1, M_CONV):
        r = pltpu.roll(zqk, k, 0)
        for t in range(k):
            r = jnp.where(row == t, prev[8 - k + t:8 - k + t + 1, :], r)
        y = y + cw[M_CONV - 1 - k:M_CONV - k, :] * r
    qk = y * _sigmoid(y)
    q = qk[:, :M_W]
    k = qk[:, M_W:] * (M_DH ** -0.5)
    k_bf = k.astype(BF16)
    v_bf = zv_ref[...].astype(BF16)

    zif = zif_ref[...]
    ic = zif[:, :LANES]
    lf = zif[:, LANES:]
    g = _dot_f32(tri_ref[...], lf)
    m_prev = m_ref[...]
    inter = g + m_prev
    dcol = ic - g
    drow = dcol.T
    gl = g[L - 1:L, :]
    wl = gl + dcol
    m_new = jnp.maximum(gl + m_prev, jnp.max(wl, axis=0, keepdims=True))
    wk = jnp.exp(wl - m_new)
    dec = jnp.exp(gl + m_prev - m_new)
    m_ref[...] = m_new

    c_old = c_ref[...]
    qc = _dot(q.astype(BF16), c_old.astype(BF16))

    head_of_lane = lax.broadcasted_iota(jnp.int32, (1, M_W), 1) // M_DH
    rt = lax.broadcasted_iota(jnp.int32, (L, L), 0)
    cs = lax.broadcasted_iota(jnp.int32, (L, L), 1)
    causal = cs <= rt
    hc = jnp.zeros((L, M_W), F32)
    wk_exp = jnp.zeros((L, M_W), F32)
    for h in range(M_HEADS):
        hmask = head_of_lane == h
        qh = jnp.where(hmask, q, 0.0).astype(BF16)
        s = lax.dot_general(qh, k_bf, (((1,), (1,)), ((), ())), preferred_element_type=F32)
        logd = jnp.where(causal, g[:, h:h + 1] + drow[h:h + 1, :], NEG_INF)
        inter_h = inter[:, h:h + 1]
        mt = jnp.maximum(inter_h, jnp.max(logd, axis=1, keepdims=True))
        w = s * jnp.exp(logd - mt)
        a = jnp.exp(inter_h - mt)
        num = _dot(w.astype(BF16), v_bf) + a * qc[:, :M_W]
        den = jnp.sum(w, axis=1, keepdims=True) + a * qc[:, M_W + h:M_W + h + 1]
        dn = jnp.maximum(jnp.abs(den), jnp.exp(-mt))
        hc = jnp.where(hmask, num / dn, hc)
        wk_exp = jnp.where(hmask, wk[:, h:h + 1], wk_exp)

    kw = (k * wk_exp).astype(BF16)
    v_aug = jnp.concatenate([v_bf, jnp.ones((L, LANES), BF16)], axis=1)
    dc = lax.dot_general(kw, v_aug, (((0,), (0,)), ((), ())), preferred_element_type=F32)
    cshape = (M_W, M_W + LANES)
    rhead = lax.broadcasted_iota(jnp.int32, cshape, 0) // M_DH
    col = lax.broadcasted_iota(jnp.int32, cshape, 1)
    chead = jnp.where(col < M_W, col // M_DH, col - M_W)
    col1 = lax.broadcasted_iota(jnp.int32, (1, M_W + LANES), 1)
    chead1 = jnp.where(col1 < M_W, col1 // M_DH, col1 - M_W)
    dec_exp = jnp.zeros((1, M_W + LANES), F32)
    for h in range(M_HEADS):
        dec_exp = jnp.where(chead1 == h, dec[:, h:h + 1], dec_exp)
    c_ref[...] = jnp.where(rhead == chead, dec_exp * c_old + dc, 0.0)

    hn = hc * lax.rsqrt(_group_sumsq(hc, ones_ref[...]) * (1.0 / M_DH) + EPS) * ng_ref[...]
    out_ref[...] = hn * zo_ref[...]


def _mlstm(zqk, zv, zo, zif, m_conv, m_norm_g, tri, ones_bd, layer, batch, seq):
    m = zqk.shape[0]
    nc = seq // L_CHUNK
    row = lambda b, c: (b * nc + c, 0)
    return pl.pallas_call(
        _mlstm_kernel,
        out_shape=jax.ShapeDtypeStruct((m, M_W), F32),
        grid=(batch, nc),
        in_specs=[
            pl.BlockSpec((L_CHUNK, 2 * M_W), row),
            pl.BlockSpec((L_CHUNK, M_W), row),
            pl.BlockSpec((L_CHUNK, M_W), row),
            pl.BlockSpec((L_CHUNK, 2 * LANES), row),
            pl.BlockSpec((None, M_CONV, 2 * M_W), lambda b, c: (layer, 0, 0)),
            pl.BlockSpec((None, 1, M_W), lambda b, c: (layer, 0, 0)),
            pl.BlockSpec((L_CHUNK, L_CHUNK), lambda b, c: (0, 0)),
            pl.BlockSpec((M_W, M_W), lambda b, c: (0, 0)),
        ],
        out_specs=pl.BlockSpec((L_CHUNK, M_W), row),
        scratch_shapes=[
            pltpu.VMEM((8, 2 * M_W), F32),
            pltpu.VMEM((M_W, M_W + LANES), F32),
            pltpu.VMEM((1, LANES), F32),
        ],
        compiler_params=pltpu.CompilerParams(dimension_semantics=("arbitrary", "arbitrary")),
        name="mlstm",
    )(zqk, zv, zo, zif, m_conv, m_norm_g, tri, ones_bd)


def _moba_prep_kernel(aq_ref, av_ref, kmean_ref, qt_ref, vt_ref, bias_ref, *, nb):
    own = pl.program_id(1)
    qt = aq_ref[...].T
    qt_ref[...] = (qt * (A_DH ** -0.5)).astype(BF16)
    vt_ref[...] = av_ref[...].T.astype(BF16)
    km = kmean_ref[...]
    blk = lax.broadcasted_iota(jnp.int32, (nb, TQ), 0).astype(F32)
    lane_head = lax.broadcasted_iota(jnp.int32, (nb, LANES), 1) // A_DH
    for h in range(A_HEADS):
        hp, hh = divmod(h, 2)
        km_h = jnp.where(lane_head == hh, km[:, hp * LANES:(hp + 1) * LANES], 0.0)
        gate = _dot_f32(km_h, qt[hp * LANES:(hp + 1) * LANES, :])
        gate = jnp.where(blk < own.astype(F32), gate, NEG_INF)
        sel = jnp.zeros((nb, TQ), jnp.bool_)
        for _ in range(MOBA_TOPK):
            mx = jnp.max(gate, axis=0, keepdims=True)
            is_max = jnp.logical_and(gate == mx, gate > NEG_INF)
            idx = jnp.min(jnp.where(is_max, blk, float(nb)), axis=0, keepdims=True)
            pick = blk == idx
            sel = jnp.logical_or(sel, pick)
            gate = jnp.where(pick, NEG_INF, gate)
        bias_ref[h] = jnp.where(sel, 0.0, NEG_INF)


def _moba_prep(aq, av, kmean, batch, seq):
    nb = seq // MOBA_BLOCK
    return pl.pallas_call(
        functools.partial(_moba_prep_kernel, nb=nb),
        out_shape=(
            jax.ShapeDtypeStruct((batch, A_W, seq), BF16),
            jax.ShapeDtypeStruct((batch, nb, A_W, MOBA_BLOCK), BF16),
            jax.ShapeDtypeStruct((batch, A_HEADS, nb, seq), F32),
        ),
        grid=(batch, nb),
        in_specs=[
            pl.BlockSpec((TQ, A_W), lambda b, i: (b * nb + i, 0)),
            pl.BlockSpec((TQ, A_W), lambda b, i: (b * nb + i, 0)),
            pl.BlockSpec((None, nb, A_W), lambda b, i: (b, 0, 0)),
        ],
        out_specs=(
            pl.BlockSpec((None, A_W, TQ), lambda b, i: (b, 0, i)),
            pl.BlockSpec((None, None, A_W, MOBA_BLOCK), lambda b, i: (b, i, 0, 0)),
            pl.BlockSpec((None, A_HEADS, nb, TQ), lambda b, i: (b, 0, 0, i)),
        ),
        compiler_params=pltpu.CompilerParams(dimension_semantics=("arbitrary", "arbitrary")),
        name="moba_prep",
    )(aq, av, kmean)


def _moba_attn_kernel(qt_ref, k_ref, vt_ref, bias_ref, o_ref):
    i = pl.program_id(2)
    qt = qt_ref[...]
    rk = lax.broadcasted_iota(jnp.int32, (MOBA_BLOCK, TQ), 0)
    cq = lax.broadcasted_iota(jnp.int32, (MOBA_BLOCK, TQ), 1)
    causal = rk <= cq
    outs = []
    for hh in range(2):
        zero_half = jnp.zeros((A_DH, TQ), BF16)
        qm = (jnp.concatenate([qt[:A_DH, :], zero_half], axis=0) if hh == 0
              else jnp.concatenate([zero_half, qt[A_DH:, :]], axis=0))

        def scores(j):
            kb = k_ref[pl.ds(pl.multiple_of(j * MOBA_BLOCK, MOBA_BLOCK), MOBA_BLOCK), :]
            return _dot(kb, qm)

        def values(j):
            return vt_ref[j, hh * A_DH:(hh + 1) * A_DH, :]

        s = jnp.where(causal, scores(i), NEG_INF)
        m = jnp.max(s, axis=0, keepdims=True)
        p = jnp.exp(s - m)
        l = jnp.sum(p, axis=0, keepdims=True)
        acc = _dot(values(i), p.astype(BF16))

        def body(j, carry):
            m, l, acc = carry
            s = scores(j) + bias_ref[hh, pl.ds(j, 1), :]
            m_new = jnp.maximum(m, jnp.max(s, axis=0, keepdims=True))
            alpha = jnp.exp(m - m_new)
            p = jnp.exp(s - m_new)
            l = alpha * l + jnp.sum(p, axis=0, keepdims=True)
            acc = alpha * acc + _dot(values(j), p.astype(BF16))
            return m_new, l, acc

        m, l, acc = lax.fori_loop(0, i, body, (m, l, acc))
        outs.append(acc / l)
    o_ref[...] = jnp.concatenate(outs, axis=0).T


def _moba_attn(qt, k, vt, bias, batch, seq):
    nb = seq // MOBA_BLOCK
    hp = A_HEADS // 2
    return pl.pallas_call(
        _moba_attn_kernel,
        out_shape=jax.ShapeDtypeStruct((batch, seq, A_W), F32),
        grid=(batch, hp, nb),
        in_specs=[
            pl.BlockSpec((None, LANES, TQ), lambda b, p, i: (b, p, i)),
            pl.BlockSpec((None, seq, LANES), lambda b, p, i: (b, 0, p)),
            pl.BlockSpec((None, nb, LANES, MOBA_BLOCK), lambda b, p, i: (b, 0, p, 0)),
            pl.BlockSpec((None, 2, nb, TQ), lambda b, p, i: (b, p, 0, i)),
        ],
        out_specs=pl.BlockSpec((None, TQ, LANES), lambda b, p, i: (b, i, p)),
        compiler_params=pltpu.CompilerParams(
            dimension_semantics=("arbitrary", "arbitrary", "arbitrary"), vmem_limit_bytes=VMEM_LIMIT),
        name="moba_attn",
    )(qt, k, vt, bias)


def _ffn_kernel(x_ref, po_ref, hm_ref, ao_ref, wout_ref, g2_ref, wg_ref, wu_ref, cg_ref, cu_ref, wd_ref,
                out_ref, xacc_ref, h2_ref, halo_g_ref, halo_u_ref, *, tiles_per_seq):
    i = pl.program_id(0)
    k = pl.program_id(1)

    @pl.when(k == 0)
    def _():
        x1 = (x_ref[...]
              + _dot(po_ref[...].astype(BF16), wout_ref[0:POOL_W, :])
              + _dot(hm_ref[...].astype(BF16), wout_ref[POOL_W:POOL_W + M_W, :])
              + _dot(ao_ref[...].astype(BF16), wout_ref[POOL_W + M_W:, :]))
        xacc_ref[...] = x1
        ms = jnp.mean(x1 * x1, axis=-1, keepdims=True)
        h2_ref[...] = (x1 * lax.rsqrt(ms + EPS) * g2_ref[...]).astype(BF16)

    h2 = h2_ref[...]

    @pl.when((i % tiles_per_seq) == 0)
    def _():
        halo_g_ref[k] = jnp.zeros((8, TF_FFN), F32)
        halo_u_ref[k] = jnp.zeros((8, TF_FFN), F32)

    row = lax.broadcasted_iota(jnp.int32, (TM_FFN, TF_FFN), 0)

    def conv(y, halo_ref, c_ref):
        prev = halo_ref[k]
        halo_ref[k] = y[TM_FFN - 8:, :]
        y1 = jnp.where(row == 0, prev[7:8, :], pltpu.roll(y, 1, 0))
        y2 = pltpu.roll(y, 2, 0)
        y2 = jnp.where(row == 0, prev[6:7, :], y2)
        y2 = jnp.where(row == 1, prev[7:8, :], y2)
        c = c_ref[...]
        return c[0:1, :] * y2 + c[1:2, :] * y1 + c[2:3, :] * y

    gc = conv(_dot(h2, wg_ref[...]), halo_g_ref, cg_ref)
    uc = conv(_dot(h2, wu_ref[...]), halo_u_ref, cu_ref)
    act = (gc * _sigmoid(gc) * uc).astype(BF16)
    xacc_ref[...] += _dot(act, wd_ref[...])

    @pl.when(k == NF_FFN - 1)
    def _():
        out_ref[...] = xacc_ref[...]


def _ffn(x2, po, hm, ao, w_out, ln2_g, w_up, ffn_conv, w_down, layer, seq):
    m = x2.shape[0]
    row = lambda i, k: (i, 0)
    return pl.pallas_call(
        functools.partial(_ffn_kernel, tiles_per_seq=seq // TM_FFN),
        out_shape=jax.ShapeDtypeStruct((m, D_MODEL), F32),
        grid=(m // TM_FFN, NF_FFN),
        in_specs=[
            pl.BlockSpec((TM_FFN, D_MODEL), row),
            pl.BlockSpec((TM_FFN, POOL_W), row),
            pl.BlockSpec((TM_FFN, M_W), row),
            pl.BlockSpec((TM_FFN, A_W), row),
            pl.BlockSpec((None, D_MODEL, D_MODEL), lambda i, k: (layer, 0, 0)),
            pl.BlockSpec((None, 1, D_MODEL), lambda i, k: (layer, 0, 0)),
            pl.BlockSpec((None, D_MODEL, TF_FFN), lambda i, k: (layer, 0, k)),
            pl.BlockSpec((None, D_MODEL, TF_FFN), lambda i, k: (layer, 0, k + NF_FFN)),
            pl.BlockSpec((None, FFN_CONV, TF_FFN), lambda i, k: (layer, 0, k)),
            pl.BlockSpec((None, FFN_CONV, TF_FFN), lambda i, k: (layer, 0, k + NF_FFN)),
            pl.BlockSpec((None, TF_FFN, D_MODEL), lambda i, k: (layer, k, 0)),
        ],
        out_specs=pl.BlockSpec((TM_FFN, D_MODEL), row),
        scratch_shapes=[
            pltpu.VMEM((TM_FFN, D_MODEL), F32),
            pltpu.VMEM((TM_FFN, D_MODEL), BF16),
            pltpu.VMEM((NF_FFN, 8, TF_FFN), F32),
            pltpu.VMEM((NF_FFN, 8, TF_FFN), F32),
        ],
        compiler_params=pltpu.CompilerParams(
            dimension_semantics=("arbitrary", "arbitrary"), vmem_limit_bytes=VMEM_LIMIT),
        name="ffn",
    )(x2, po, hm, ao, w_out, ln2_g, w_up, w_up, ffn_conv, ffn_conv, w_down)


def _block_diag_ones(n, group):
    idx = jnp.arange(n) // group
    return (idx[:, None] == idx[None, :]).astype(BF16)


def kernel(x, ln1_g, w_in, pool_w, pool_scale, m_conv, m_b_i, m_b_f, m_norm_g, a_q_g, a_k_g, w_out, ln2_g,
           w_up, ffn_conv, w_down):
    batch, seq, _ = x.shape
    depth = w_in.shape[0]
    assert seq % TM_IN == 0 and seq % TM_FFN == 0 and seq % TP_POOL == 0 and seq % MOBA_BLOCK == 0

    def pad_lanes(w):
        return jnp.pad(w, ((0, 0), (0, 0), (0, LANES - w.shape[-1])))

    w_in_p = jnp.concatenate([
        w_in[:, :, :OFF_MI],
        w_in[:, :, OFF_A:],
        pad_lanes(w_in[:, :, OFF_MI:OFF_MF]),
        pad_lanes(w_in[:, :, OFF_MF:OFF_A]),
    ], axis=-1).astype(BF16)
    bif = jnp.concatenate([pad_lanes(m_b_i[:, None, :]), pad_lanes(m_b_f[:, None, :])], axis=-1)
    gq = jnp.tile(a_q_g, (1, A_HEADS))[:, None, :]
    gk = jnp.tile(a_k_g, (1, A_HEADS))[:, None, :]
    eye_g = jnp.eye(len(POOL_WINDOWS), dtype=F32)
    pool_bd = jnp.einsum("lgcd,gh->lgchd", pool_w, eye_g).reshape(depth, POOL_W, POOL_W).astype(BF16)
    ones_a = _block_diag_ones(A_W, A_DH)
    ones_m = _block_diag_ones(M_W, M_DH)
    tri = (jnp.arange(L_CHUNK)[:, None] >= jnp.arange(L_CHUNK)[None, :]).astype(F32)
    w_out_b = w_out.astype(BF16)
    w_up_b = w_up.astype(BF16)
    w_down_b = w_down.astype(BF16)

    x2 = x.reshape(batch * seq, D_MODEL)
    for l in range(depth):
        zp, zqk, zv, zo, aq, ak, av, zif, kmean = _in_proj(
            x2, ln1_g[:, None, :], w_in_p, bif, gq, gk, ones_a, l)
        po = _pool(zp, pool_bd, pool_scale[:, None, :], l, seq)
        hm = _mlstm(zqk, zv, zo, zif, m_conv, m_norm_g[:, None, :], tri, ones_m, l, batch, seq)
        qt, vt, bias = _moba_prep(aq, av, kmean.reshape(batch, seq // MOBA_BLOCK, A_W), batch, seq)
        ao = _moba_attn(qt, ak.reshape(batch, seq, A_W), vt, bias, batch, seq)
        x2 = _ffn(x2, po, hm, ao.reshape(batch * seq, A_W), w_out_b, ln2_g[:, None, :], w_up_b, ffn_conv,
                  w_down_b, l, seq)
    return x2.reshape(batch, seq, D_MODEL)
```

```python
import functools

import jax
import jax.numpy as jnp
from jax import lax
from jax.experimental import pallas as pl
from jax.experimental.pallas import tpu as pltpu

F32 = jnp.float32
BF16 = jnp.bfloat16

D_MODEL = 1024
DEPTH = 4
POOL_WINDOWS = (2, 4, 8, 16)
POOL_GC = 64
POOL_W = 256
M_HEADS = 4
M_DH = 64
M_W = 256
M_CONV = 4
A_HEADS = 8
A_DH = 64
A_W = 512
MOBA_BLOCK = 256
MOBA_TOPK = 3
D_FF = 2816
FFN_CONV = 3
EPS = 1e-6

OFF_MI = POOL_W + 4 * M_W
OFF_MF = OFF_MI + M_HEADS
OFF_A = OFF_MF + M_HEADS

C_POOL = 0
C_MQK = 256
C_MV = 768
C_MO = 1024
C_AQ = 1280
C_AK = 1792
C_AV = 2304
C_MI = 2816
C_MF = 2944
IN_WP = 3072

LANES = 128
VMEM_LIMIT = 56 * 1024 * 1024

TM_IN = 512
TP_POOL = 512
POOL_HALO = 16
L_CHUNK = 128
TQ = MOBA_BLOCK
ATTN_GROUP = 4
V_ROWS = A_DH + 16
LOG2_E = 1.4426950408889634
TM_FFN = 1024
TF_FFN = 256
NF_FFN = D_FF // TF_FFN

NEG_INF = float("-inf")


def _dot(a, b):
    return jnp.dot(a, b, preferred_element_type=F32)


def _dot_f32(a, b):
    return jnp.dot(a, b, preferred_element_type=F32, precision=lax.Precision.HIGHEST)


def _group_sumsq(z, ones_bd):
    sq = z * z
    hi = sq.astype(BF16)
    lo = (sq - hi.astype(F32)).astype(BF16)
    return _dot(hi, ones_bd) + _dot(lo, ones_bd)


def _log_sigmoid(x):
    return jnp.minimum(x, 0.0) - jnp.log1p(jnp.exp(-jnp.abs(x)))


def _sigmoid(x):
    return 1.0 / (1.0 + jnp.exp(-x))


def _in_proj_kernel(x_ref, g_ref, w_ref, bif_ref, gq_ref, gk_ref, ones_ref,
                    zp_ref, zqk_ref, zv_ref, zo_ref, aq_ref, ak_ref, av_ref, zif_ref, kmean_ref):
    xf = x_ref[...]
    ms = jnp.mean(xf * xf, axis=-1, keepdims=True)
    h = (xf * lax.rsqrt(ms + EPS) * g_ref[...]).astype(BF16)

    def proj(a, b):
        return _dot(h, w_ref[:, a:b])

    zp_ref[...] = proj(C_POOL, C_MQK)
    zqk_ref[...] = proj(C_MQK, C_MV)
    zv_ref[...] = proj(C_MV, C_MO)
    zo_ref[...] = _sigmoid(proj(C_MO, C_AQ))

    ones_bd = ones_ref[...]
    zq = proj(C_AQ, C_AK)
    aq_ref[...] = zq * lax.rsqrt(_group_sumsq(zq, ones_bd) * (1.0 / A_DH) + EPS) * gq_ref[...]
    zk = proj(C_AK, C_AV)
    ak = zk * lax.rsqrt(_group_sumsq(zk, ones_bd) * (1.0 / A_DH) + EPS) * gk_ref[...]
    ak_ref[...] = ak.astype(BF16)
    for blk in range(TM_IN // MOBA_BLOCK):
        kmean_ref[0, blk:blk + 1, :] = jnp.mean(
            ak[blk * MOBA_BLOCK:(blk + 1) * MOBA_BLOCK, :], axis=0, keepdims=True)
    av_ref[...] = proj(C_AV, C_MI)

    zif = proj(C_MI, IN_WP) + bif_ref[...]
    zif_ref[:, :LANES] = zif[:, :LANES]
    zif_ref[:, LANES:] = _log_sigmoid(zif[:, LANES:])


def _in_proj(x2, ln1_g, w_in_p, bif, gq, gk, ones_bd, layer):
    m = x2.shape[0]
    grid = (m // TM_IN,)
    row = lambda i: (i, 0)
    const = lambda i: (0, 0)
    out_shape = (
        jax.ShapeDtypeStruct((m, POOL_W), F32),
        jax.ShapeDtypeStruct((m, 2 * M_W), F32),
        jax.ShapeDtypeStruct((m, M_W), F32),
        jax.ShapeDtypeStruct((m, M_W), F32),
        jax.ShapeDtypeStruct((m, A_W), F32),
        jax.ShapeDtypeStruct((m, A_W), BF16),
        jax.ShapeDtypeStruct((m, A_W), F32),
        jax.ShapeDtypeStruct((m, 2 * LANES), F32),
        jax.ShapeDtypeStruct((m // TM_IN, TM_IN // MOBA_BLOCK, A_W), F32),
    )
    out_specs = (
        pl.BlockSpec((TM_IN, POOL_W), row),
        pl.BlockSpec((TM_IN, 2 * M_W), row),
        pl.BlockSpec((TM_IN, M_W), row),
        pl.BlockSpec((TM_IN, M_W), row),
        pl.BlockSpec((TM_IN, A_W), row),
        pl.BlockSpec((TM_IN, A_W), row),
        pl.BlockSpec((TM_IN, A_W), row),
        pl.BlockSpec((TM_IN, 2 * LANES), row),
        pl.BlockSpec((1, TM_IN // MOBA_BLOCK, A_W), lambda i: (i, 0, 0)),
    )
    in_specs = [
        pl.BlockSpec((TM_IN, D_MODEL), row),
        pl.BlockSpec((None, 1, D_MODEL), lambda i: (layer, 0, 0)),
        pl.BlockSpec((None, D_MODEL, IN_WP), lambda i: (layer, 0, 0)),
        pl.BlockSpec((None, 1, 2 * LANES), lambda i: (layer, 0, 0)),
        pl.BlockSpec((None, 1, A_W), lambda i: (layer, 0, 0)),
        pl.BlockSpec((None, 1, A_W), lambda i: (layer, 0, 0)),
        pl.BlockSpec((A_W, A_W), const),
    ]
    return pl.pallas_call(
        _in_proj_kernel,
        out_shape=out_shape,
        grid=grid,
        in_specs=in_specs,
        out_specs=out_specs,
        compiler_params=pltpu.CompilerParams(
            dimension_semantics=("arbitrary",), vmem_limit_bytes=VMEM_LIMIT),
        name="in_proj",
    )(x2, ln1_g, w_in_p, bif, gq, gk, ones_bd)


def _pool_kernel(zp_ref, wbd_ref, scale_ref, out_ref, halo_ref, *, tiles_per_seq):
    i = pl.program_id(0)
    first = (i % tiles_per_seq) == 0
    x = zp_ref[...]

    @pl.when(first)
    def _():
        halo_ref[...] = jnp.zeros_like(halo_ref)

    prev = halo_ref[...]
    halo_ref[...] = x[TP_POOL - POOL_HALO:, :]
    xe = jnp.concatenate([prev, x], axis=0)
    s2 = xe + pltpu.roll(xe, 1, 0)
    s4 = s2 + pltpu.roll(s2, 2, 0)
    s8 = s4 + pltpu.roll(s4, 4, 0)
    s16 = s8 + pltpu.roll(s8, 8, 0)
    shape = (TP_POOL, POOL_W)
    group = lax.broadcasted_iota(jnp.int32, shape, 1) // POOL_GC
    pos = (i % tiles_per_seq) * TP_POOL + lax.broadcasted_iota(jnp.int32, shape, 0)
    wsum = s16[POOL_HALO:, :]
    wsize = jnp.full(shape, POOL_WINDOWS[3], jnp.int32)
    for g, s in ((2, s8), (1, s4), (0, s2)):
        wsum = jnp.where(group == g, s[POOL_HALO:, :], wsum)
        wsize = jnp.where(group == g, POOL_WINDOWS[g], wsize)
    cnt = jnp.minimum(pos + 1, wsize).astype(F32)
    d = wsum / cnt - x
    out_ref[...] = _dot(d.astype(BF16), wbd_ref[...]) * scale_ref[...]


def _pool(zp, wbd, scale, layer, seq):
    m = zp.shape[0]
    row = lambda i: (i, 0)
    return pl.pallas_call(
        functools.partial(_pool_kernel, tiles_per_seq=seq // TP_POOL),
        out_shape=jax.ShapeDtypeStruct((m, POOL_W), F32),
        grid=(m // TP_POOL,),
        in_specs=[
            pl.BlockSpec((TP_POOL, POOL_W), row),
            pl.BlockSpec((None, POOL_W, POOL_W), lambda i: (layer, 0, 0)),
            pl.BlockSpec((None, 1, POOL_W), lambda i: (layer, 0, 0)),
        ],
        out_specs=pl.BlockSpec((TP_POOL, POOL_W), row),
        scratch_shapes=[pltpu.VMEM((POOL_HALO, POOL_W), F32)],
        compiler_params=pltpu.CompilerParams(dimension_semantics=("arbitrary",)),
        name="pool",
    )(zp, wbd, scale)


def _mlstm_kernel(zqk_ref, zv_ref, zo_ref, zif_ref, conv_ref, ng_ref, tri_ref, ones_ref, out_ref,
                  halo_ref, c_ref, m_ref):
    L = L_CHUNK
    ci = pl.program_id(1)

    @pl.when(ci == 0)
    def _():
        halo_ref[...] = jnp.zeros_like(halo_ref)
        c_ref[...] = jnp.zeros_like(c_ref)
        m_ref[...] = jnp.zeros_like(m_ref)

    zqk = zqk_ref[...]
    prev = halo_ref[...]
    halo_ref[...] = zqk[L - 8:, :]
    row = lax.broadcasted_iota(jnp.int32, (L, 2 * M_W), 0)
    cw = conv_ref[...]
    y = cw[M_CONV - 1:M_CONV, :] * zqk
    for k in range(1, M_CONV):
        r = pltpu.roll(zqk, k, 0)
        for t in range(k):
            r = jnp.where(row == t, prev[8 - k + t:8 - k + t + 1, :], r)
        y = y + cw[M_CONV - 1 - k:M_CONV - k, :] * r
    qk = y * _sigmoid(y)
    q = qk[:, :M_W]
    k = qk[:, M_W:] * (M_DH ** -0.5)
    k_bf = k.astype(BF16)
    v_bf = zv_ref[...].astype(BF16)

    zif = zif_ref[...]
    ic = zif[:, :LANES]
    lf = zif[:, LANES:]
    g = _dot_f32(tri_ref[...], lf)
    m_prev = m_ref[...]
    inter = g + m_prev
    dcol = ic - g
    drow = dcol.T
    gl = g[L - 1:L, :]
    wl = gl + dcol
    m_new = jnp.maximum(gl + m_prev, jnp.max(wl, axis=0, keepdims=True))
    wk = jnp.exp(wl - m_new)
    dec = jnp.exp(gl + m_prev - m_new)
    m_ref[...] = m_new

    c_old = c_ref[...]
    qc = _dot(q.astype(BF16), c_old.astype(BF16))

    head_of_lane = lax.broadcasted_iota(jnp.int32, (1, M_W), 1) // M_DH
    rt = lax.broadcasted_iota(jnp.int32, (L, L), 0)
    cs = lax.broadcasted_iota(jnp.int32, (L, L), 1)
    causal = cs <= rt
    hc = jnp.zeros((L, M_W), F32)
    wk_exp = jnp.zeros((L, M_W), F32)
    for h in range(M_HEADS):
        hmask = head_of_lane == h
        qh = jnp.where(hmask, q, 0.0).astype(BF16)
        s = lax.dot_general(qh, k_bf, (((1,), (1,)), ((), ())), preferred_element_type=F32)
        logd = jnp.where(causal, g[:, h:h + 1] + drow[h:h + 1, :], NEG_INF)
        inter_h = inter[:, h:h + 1]
        mt = jnp.maximum(inter_h, jnp.max(logd, axis=1, keepdims=True))
        w = s * jnp.exp(logd - mt)
        a = jnp.exp(inter_h - mt)
        num = _dot(w.astype(BF16), v_bf) + a * qc[:, :M_W]
        den = jnp.sum(w, axis=1, keepdims=True) + a * qc[:, M_W + h:M_W + h + 1]
        dn = jnp.maximum(jnp.abs(den), jnp.exp(-mt))
        hc = jnp.where(hmask, num / dn, hc)
        wk_exp = jnp.where(hmask, wk[:, h:h + 1], wk_exp)

    kw = (k * wk_exp).astype(BF16)
    v_aug = jnp.concatenate([v_bf, jnp.ones((L, LANES), BF16)], axis=1)
    dc = lax.dot_general(kw, v_aug, (((0,), (0,)), ((), ())), preferred_element_type=F32)
    cshape = (M_W, M_W + LANES)
    rhead = lax.broadcasted_iota(jnp.int32, cshape, 0) // M_DH
    col = lax.broadcasted_iota(jnp.int32, cshape, 1)
    chead = jnp.where(col < M_W, col // M_DH, col - M_W)
    col1 = lax.broadcasted_iota(jnp.int32, (1, M_W + LANES), 1)
    chead1 = jnp.where(col1 < M_W, col1 // M_DH, col1 - M_W)
    dec_exp = jnp.zeros((1, M_W + LANES), F32)
    for h in range(M_HEADS):
        dec_exp = jnp.where(chead1 == h, dec[:, h:h + 1], dec_exp)
    c_ref[...] = jnp.where(rhead == chead, dec_exp * c_old + dc, 0.0)

    hn = hc * lax.rsqrt(_group_sumsq(hc, ones_ref[...]) * (1.0 / M_DH) + EPS) * ng_ref[...]
    out_ref[...] = hn * zo_ref[...]


def _mlstm(zqk, zv, zo, zif, m_conv, m_norm_g, tri, ones_bd, layer, batch, seq):
    m = zqk.shape[0]
    nc = seq // L_CHUNK
    row = lambda b, c: (b * nc + c, 0)
    return pl.pallas_call(
        _mlstm_kernel,
        out_shape=jax.ShapeDtypeStruct((m, M_W), F32),
        grid=(batch, nc),
        in_specs=[
            pl.BlockSpec((L_CHUNK, 2 * M_W), row),
            pl.BlockSpec((L_CHUNK, M_W), row),
            pl.BlockSpec((L_CHUNK, M_W), row),
            pl.BlockSpec((L_CHUNK, 2 * LANES), row),
            pl.BlockSpec((None, M_CONV, 2 * M_W), lambda b, c: (layer, 0, 0)),
            pl.BlockSpec((None, 1, M_W), lambda b, c: (layer, 0, 0)),
            pl.BlockSpec((L_CHUNK, L_CHUNK), lambda b, c: (0, 0)),
            pl.BlockSpec((M_W, M_W), lambda b, c: (0, 0)),
        ],
        out_specs=pl.BlockSpec((L_CHUNK, M_W), row),
        scratch_shapes=[
            pltpu.VMEM((8, 2 * M_W), F32),
            pltpu.VMEM((M_W, M_W + LANES), F32),
            pltpu.VMEM((1, LANES), F32),
        ],
        compiler_params=pltpu.CompilerParams(dimension_semantics=("arbitrary", "arbitrary")),
        name="mlstm",
    )(zqk, zv, zo, zif, m_conv, m_norm_g, tri, ones_bd)


def _moba_prep_kernel(aq_ref, av_ref, kmean_ref, qt_ref, vt_ref, bias_ref, *, nb):
    own = pl.program_id(1)
    qt = aq_ref[...].T
    qt_ref[...] = (qt * (A_DH ** -0.5 * LOG2_E)).astype(BF16)
    vt = av_ref[...].T.astype(BF16)
    ones_rows = jnp.ones((V_ROWS - A_DH, MOBA_BLOCK), BF16)
    for h in range(A_HEADS):
        vt_ref[h * V_ROWS:h * V_ROWS + A_DH, :] = vt[h * A_DH:(h + 1) * A_DH, :]
        vt_ref[h * V_ROWS + A_DH:(h + 1) * V_ROWS, :] = ones_rows
    km = kmean_ref[...]
    blk = lax.broadcasted_iota(jnp.int32, (nb, TQ), 0).astype(F32)
    lane_head = lax.broadcasted_iota(jnp.int32, (nb, LANES), 1) // A_DH
    for h in range(A_HEADS):
        hp, hh = divmod(h, 2)
        km_h = jnp.where(lane_head == hh, km[:, hp * LANES:(hp + 1) * LANES], 0.0)
        gate = _dot_f32(km_h, qt[hp * LANES:(hp + 1) * LANES, :])
        gate = jnp.where(blk < own.astype(F32), gate, NEG_INF)
        sel = jnp.zeros((nb, TQ), jnp.bool_)
        for _ in range(MOBA_TOPK):
            mx = jnp.max(gate, axis=0, keepdims=True)
            is_max = jnp.logical_and(gate == mx, gate > NEG_INF)
            idx = jnp.min(jnp.where(is_max, blk, float(nb)), axis=0, keepdims=True)
            pick = blk == idx
            sel = jnp.logical_or(sel, pick)
            gate = jnp.where(pick, NEG_INF, gate)
        bias_ref[h] = jnp.where(sel, 0.0, NEG_INF)


def _moba_prep(aq, av, kmean, batch, seq):
    nb = seq // MOBA_BLOCK
    return pl.pallas_call(
        functools.partial(_moba_prep_kernel, nb=nb),
        out_shape=(
            jax.ShapeDtypeStruct((batch, A_W, seq), BF16),
            jax.ShapeDtypeStruct((batch, nb, A_HEADS * V_ROWS, MOBA_BLOCK), BF16),
            jax.ShapeDtypeStruct((batch, A_HEADS, nb, seq), F32),
        ),
        grid=(batch, nb),
        in_specs=[
            pl.BlockSpec((TQ, A_W), lambda b, i: (b * nb + i, 0)),
            pl.BlockSpec((TQ, A_W), lambda b, i: (b * nb + i, 0)),
            pl.BlockSpec((None, nb, A_W), lambda b, i: (b, 0, 0)),
        ],
        out_specs=(
            pl.BlockSpec((None, A_W, TQ), lambda b, i: (b, 0, i)),
            pl.BlockSpec((None, None, A_HEADS * V_ROWS, MOBA_BLOCK), lambda b, i: (b, i, 0, 0)),
            pl.BlockSpec((None, A_HEADS, nb, TQ), lambda b, i: (b, 0, 0, i)),
        ),
        compiler_params=pltpu.CompilerParams(dimension_semantics=("arbitrary", "arbitrary")),
        name="moba_prep",
    )(aq, av, kmean)


def _moba_attn_kernel(qt_ref, k_ref, vt_ref, bias_ref, o_ref, s_ref):
    i = pl.program_id(2)
    own_slot = s_ref.shape[1] - 1
    qt = qt_ref[...]
    zero_half = jnp.zeros((A_DH, TQ), BF16)
    qm = (jnp.concatenate([qt[:A_DH, :], zero_half], axis=0),
          jnp.concatenate([zero_half, qt[A_DH:, :]], axis=0))

    def keys(j):
        return k_ref[pl.ds(pl.multiple_of(j * MOBA_BLOCK, MOBA_BLOCK), MOBA_BLOCK), :]

    def values(j, hh):
        return vt_ref[j, hh * V_ROWS:(hh + 1) * V_ROWS, :]

    def colmax8(s):
        return jnp.max(s.reshape(MOBA_BLOCK // 8, 8, TQ), axis=0)

    rk = lax.broadcasted_iota(jnp.int32, (MOBA_BLOCK, TQ), 0)
    cq = lax.broadcasted_iota(jnp.int32, (MOBA_BLOCK, TQ), 1)
    causal = rk <= cq
    k_own = keys(i)
    m8 = []
    for hh in range(2):
        s = jnp.where(causal, _dot(k_own, qm[hh]), NEG_INF)
        s_ref[hh, own_slot] = s
        m8.append(colmax8(s))

    n_groups = (i + ATTN_GROUP - 1) // ATTN_GROUP

    def score_pass(g, m8):
        m8 = list(m8)
        for b in range(ATTN_GROUP):
            j = g * ATTN_GROUP + b
            kb = keys(j)
            for hh in range(2):
                s = _dot(kb, qm[hh]) + bias_ref[hh, g, b:b + 1, :]
                s_ref[hh, j] = s
                m8[hh] = jnp.maximum(m8[hh], colmax8(s))
        return tuple(m8)

    m8 = lax.fori_loop(0, n_groups, score_pass, tuple(m8))
    m = [jnp.max(x, axis=0, keepdims=True) for x in m8]

    def weighted_values(j, hh):
        p = jnp.exp2(s_ref[hh, j] - m[hh])
        return _dot(values(j, hh), p.astype(BF16))

    def value_pass(g, acc):
        acc = list(acc)
        for b in range(ATTN_GROUP):
            for hh in range(2):
                acc[hh] = acc[hh] + weighted_values(g * ATTN_GROUP + b, hh)
        return tuple(acc)

    acc = tuple(_dot(values(i, hh), jnp.exp2(s_ref[hh, own_slot] - m[hh]).astype(BF16)) for hh in range(2))
    acc = lax.fori_loop(0, n_groups, value_pass, acc)
    outs = [acc[hh][:A_DH, :] / acc[hh][A_DH:A_DH + 1, :] for hh in range(2)]
    o_ref[...] = jnp.concatenate(outs, axis=0).T


def _moba_attn(qt, k, vt, bias, batch, seq):
    nb = seq // MOBA_BLOCK
    assert nb % ATTN_GROUP == 0
    hp = A_HEADS // 2
    bias = bias.reshape(batch, A_HEADS, nb // ATTN_GROUP, ATTN_GROUP, seq)
    return pl.pallas_call(
        _moba_attn_kernel,
        out_shape=jax.ShapeDtypeStruct((batch, seq, A_W), F32),
        grid=(batch, hp, nb),
        in_specs=[
            pl.BlockSpec((None, LANES, TQ), lambda b, p, i: (b, p, i)),
            pl.BlockSpec((None, seq, LANES), lambda b, p, i: (b, 0, p)),
            pl.BlockSpec((None, nb, 2 * V_ROWS, MOBA_BLOCK), lambda b, p, i: (b, 0, p, 0)),
            pl.BlockSpec((None, 2, nb // ATTN_GROUP, ATTN_GROUP, TQ), lambda b, p, i: (b, p, 0, 0, i)),
        ],
        out_specs=pl.BlockSpec((None, TQ, LANES), lambda b, p, i: (b, i, p)),
        scratch_shapes=[pltpu.VMEM((2, nb + 1, MOBA_BLOCK, TQ), F32)],
        compiler_params=pltpu.CompilerParams(
            dimension_semantics=("arbitrary", "arbitrary", "arbitrary"), vmem_limit_bytes=VMEM_LIMIT),
        name="moba_attn",
    )(qt, k, vt, bias)


def _ffn_kernel(x_ref, po_ref, hm_ref, ao_ref, wout_ref, g2_ref, wg_ref, wu_ref, cg_ref, cu_ref, wd_ref,
                out_ref, xacc_ref, h2_ref, halo_g_ref, halo_u_ref, *, tiles_per_seq):
    i = pl.program_id(0)
    k = pl.program_id(1)

    @pl.when(k == 0)
    def _():
        x1 = (x_ref[...]
              + _dot(po_ref[...].astype(BF16), wout_ref[0:POOL_W, :])
              + _dot(hm_ref[...].astype(BF16), wout_ref[POOL_W:POOL_W + M_W, :])
              + _dot(ao_ref[...].astype(BF16), wout_ref[POOL_W + M_W:, :]))
        xacc_ref[...] = x1
        ms = jnp.mean(x1 * x1, axis=-1, keepdims=True)
        h2_ref[...] = (x1 * lax.rsqrt(ms + EPS) * g2_ref[...]).astype(BF16)

    h2 = h2_ref[...]

    @pl.when((i % tiles_per_seq) == 0)
    def _():
        halo_g_ref[k] = jnp.zeros((8, TF_FFN), F32)
        halo_u_ref[k] = jnp.zeros((8, TF_FFN), F32)

    def conv(y, halo_ref, c_ref):
        prev = halo_ref[k]
        halo_ref[k] = y[TM_FFN - 8:, :]
        c = c_ref[...]

        def taps(a):
            z = c[1:2, :] * a + c[0:1, :] * pltpu.roll(a, 1, 0)
            return c[2:3, :] * a + pltpu.roll(z, 1, 0)

        head = taps(jnp.concatenate([prev, y[:8, :]], axis=0))[8:, :]
        return jnp.concatenate([head, taps(y)[8:, :]], axis=0)

    gc = conv(_dot(h2, wg_ref[...]), halo_g_ref, cg_ref)
    uc = conv(_dot(h2, wu_ref[...]), halo_u_ref, cu_ref)
    act = (gc * _sigmoid(gc) * uc).astype(BF16)
    xacc_ref[...] += _dot(act, wd_ref[...])

    @pl.when(k == NF_FFN - 1)
    def _():
        out_ref[...] = xacc_ref[...]


def _ffn(x2, po, hm, ao, w_out, ln2_g, w_up, ffn_conv, w_down, layer, seq):
    m = x2.shape[0]
    row = lambda i, k: (i, 0)
    return pl.pallas_call(
        functools.partial(_ffn_kernel, tiles_per_seq=seq // TM_FFN),
        out_shape=jax.ShapeDtypeStruct((m, D_MODEL), F32),
        grid=(m // TM_FFN, NF_FFN),
        in_specs=[
            pl.BlockSpec((TM_FFN, D_MODEL), row),
            pl.BlockSpec((TM_FFN, POOL_W), row),
            pl.BlockSpec((TM_FFN, M_W), row),
            pl.BlockSpec((TM_FFN, A_W), row),
            pl.BlockSpec((None, D_MODEL, D_MODEL), lambda i, k: (layer, 0, 0)),
            pl.BlockSpec((None, 1, D_MODEL), lambda i, k: (layer, 0, 0)),
            pl.BlockSpec((None, D_MODEL, TF_FFN), lambda i, k: (layer, 0, k)),
            pl.BlockSpec((None, D_MODEL, TF_FFN), lambda i, k: (layer, 0, k + NF_FFN)),
            pl.BlockSpec((None, FFN_CONV, TF_FFN), lambda i, k: (layer, 0, k)),
            pl.BlockSpec((None, FFN_CONV, TF_FFN), lambda i, k: (layer, 0, k + NF_FFN)),
            pl.BlockSpec((None, TF_FFN, D_MODEL), lambda i, k: (layer, k, 0)),
        ],
        out_specs=pl.BlockSpec((TM_FFN, D_MODEL), row),
        scratch_shapes=[
            pltpu.VMEM((TM_FFN, D_MODEL), F32),
            pltpu.VMEM((TM_FFN, D_MODEL), BF16),
            pltpu.VMEM((NF_FFN, 8, TF_FFN), F32),
            pltpu.VMEM((NF_FFN, 8, TF_FFN), F32),
        ],
        compiler_params=pltpu.CompilerParams(
            dimension_semantics=("arbitrary", "arbitrary"), vmem_limit_bytes=VMEM_LIMIT),
        name="ffn",
    )(x2, po, hm, ao, w_out, ln2_g, w_up, w_up, ffn_conv, ffn_conv, w_down)


def _block_diag_ones(n, group):
    idx = jnp.arange(n) // group
    return (idx[:, None] == idx[None, :]).astype(BF16)


def kernel(x, ln1_g, w_in, pool_w, pool_scale, m_conv, m_b_i, m_b_f, m_norm_g, a_q_g, a_k_g, w_out, ln2_g,
           w_up, ffn_conv, w_down):
    batch, seq, _ = x.shape
    depth = w_in.shape[0]
    assert seq % TM_IN == 0 and seq % TM_FFN == 0 and seq % TP_POOL == 0 and seq % MOBA_BLOCK == 0

    def pad_lanes(w):
        return jnp.pad(w, ((0, 0), (0, 0), (0, LANES - w.shape[-1])))

    w_in_p = jnp.concatenate([
        w_in[:, :, :OFF_MI],
        w_in[:, :, OFF_A:],
        pad_lanes(w_in[:, :, OFF_MI:OFF_MF]),
        pad_lanes(w_in[:, :, OFF_MF:OFF_A]),
    ], axis=-1).astype(BF16)
    bif = jnp.concatenate([pad_lanes(m_b_i[:, None, :]), pad_lanes(m_b_f[:, None, :])], axis=-1)
    gq = jnp.tile(a_q_g, (1, A_HEADS))[:, None, :]
    gk = jnp.tile(a_k_g, (1, A_HEADS))[:, None, :]
    eye_g = jnp.eye(len(POOL_WINDOWS), dtype=F32)
    pool_bd = jnp.einsum("lgcd,gh->lgchd", pool_w, eye_g).reshape(depth, POOL_W, POOL_W).astype(BF16)
    ones_a = _block_diag_ones(A_W, A_DH)
    ones_m = _block_diag_ones(M_W, M_DH)
    tri = (jnp.arange(L_CHUNK)[:, None] >= jnp.arange(L_CHUNK)[None, :]).astype(F32)
    w_out_b = w_out.astype(BF16)
    w_up_b = w_up.astype(BF16)
    w_down_b = w_down.astype(BF16)

    x2 = x.reshape(batch * seq, D_MODEL)
    for l in range(depth):
        zp, zqk, zv, zo, aq, ak, av, zif, kmean = _in_proj(
            x2, ln1_g[:, None, :], w_in_p, bif, gq, gk, ones_a, l)
        po = _pool(zp, pool_bd, pool_scale[:, None, :], l, seq)
        hm = _mlstm(zqk, zv, zo, zif, m_conv, m_norm_g[:, None, :], tri, ones_m, l, batch, seq)
        qt, vt, bias = _moba_prep(aq, av, kmean.reshape(batch, seq // MOBA_BLOCK, A_W), batch, seq)
        ao = _moba_attn(qt, ak.reshape(batch, seq, A_W), vt, bias, batch, seq)
        x2 = _ffn(x2, po, hm, ao.reshape(batch * seq, A_W), w_out_b, ln2_g[:, None, :], w_up_b, ffn_conv,
                  w_down_b, l, seq)
    return x2.reshape(batch, seq, D_MODEL)
```
